```python
import jax
import jax.numpy as jnp
from jax import lax
import numpy as np

D_MODEL = 1024
BATCH = 8
SEQ = 4096
DEPTH = 4

GRID_W = 64
CTX_LEN = 256
N_MIXERS = 2
N_GLA_LAYERS = (DEPTH + N_MIXERS - 1) // N_MIXERS
N_CONV_LAYERS = DEPTH // N_MIXERS
GLA_HEADS = 4
GLA_DK = D_MODEL // 2
GLA_DV = D_MODEL
GLA_HEAD_K = GLA_DK // GLA_HEADS
GLA_HEAD_V = GLA_DV // GLA_HEADS
GLA_GATE_RANK = 16
GLA_GATE_TAU = 16.0
GLA_CHUNK = 64
GLA_IN = 2 * GLA_DK + 2 * GLA_DV + 2 * GLA_GATE_RANK
CONV_WIDTH = 31
FFN_HIDDEN = -(-8 * D_MODEL // (3 * 256)) * 256
NORM_EPS = 1e-6

kernel_name = 'hybrid_gla_conformer_prefix_dit'


def rmsnorm(x, g):
    xf = x.astype(jnp.float32)
    y = xf * lax.rsqrt(jnp.mean(xf * xf, axis=-1, keepdims=True) + NORM_EPS)
    return (y * g.astype(jnp.float32)).astype(x.dtype)


def layernorm(x, g, b):
    xf = x.astype(jnp.float32)
    mu = jnp.mean(xf, axis=-1, keepdims=True)
    xc = xf - mu
    y = xc * lax.rsqrt(jnp.mean(xc * xc, axis=-1, keepdims=True) + NORM_EPS)
    return (y * g.astype(jnp.float32) + b.astype(jnp.float32)).astype(x.dtype)


def modulate(h, shift, scale):
    return h * (1.0 + scale) + shift


def to_grid_order(t, col_major):
    if not col_major:
        return t
    b, l, d = t.shape
    rows = l // GRID_W
    return t.reshape(b, rows, GRID_W, d).transpose(0, 2, 1, 3).reshape(b, l, d)


def from_grid_order(t, col_major):
    if not col_major:
        return t
    b, l, d = t.shape
    rows = l // GRID_W
    return t.reshape(b, GRID_W, rows, d).transpose(0, 2, 1, 3).reshape(b, l, d)


def split_heads(t, head_dim):
    b, l, _ = t.shape
    return t.reshape(b, l, GLA_HEADS, head_dim).transpose(0, 2, 1, 3)


def gla_project(h, w_in, wa_f, ba_f, wa_b, ba_b):
    z = (h @ w_in).astype(jnp.float32)
    o1 = GLA_DK
    o2 = 2 * GLA_DK
    o3 = o2 + GLA_DV
    o4 = o3 + GLA_DV
    o5 = o4 + GLA_GATE_RANK
    q = split_heads(z[..., :o1], GLA_HEAD_K) * (GLA_HEAD_K ** -0.5)
    k = split_heads(z[..., o1:o2], GLA_HEAD_K)
    v = split_heads(z[..., o2:o3], GLA_HEAD_V)
    r = z[..., o3:o4]
    lg_f = jax.nn.log_sigmoid(z[..., o4:o5] @ wa_f.astype(jnp.float32) + ba_f.astype(jnp.float32)) / GLA_GATE_TAU
    lg_b = jax.nn.log_sigmoid(z[..., o5:] @ wa_b.astype(jnp.float32) + ba_b.astype(jnp.float32)) / GLA_GATE_TAU
    return q, k, v, r, split_heads(lg_f, GLA_HEAD_K), split_heads(lg_b, GLA_HEAD_K)


def gla_chunked(q, k, v, log_g, s0, strict):
    b_, h_, l_, _ = q.shape
    dv = v.shape[-1]
    n = l_ // GLA_CHUNK
    ch = lambda t: t.reshape(b_, h_, n, GLA_CHUNK, t.shape[-1])
    q, k, v, log_g = ch(q), ch(k), ch(v), ch(log_g)
    bcum = jnp.cumsum(log_g, axis=3)
    b_last = bcum[:, :, :, -1:, :]
    qe = q * jnp.exp(bcum)
    ke = k * jnp.exp(-bcum)
    kd = k * jnp.exp(b_last - bcum)
    mask = jnp.tril(jnp.ones((GLA_CHUNK, GLA_CHUNK), dtype=bool), k=-1 if strict else 0)
    a = jnp.where(mask, jnp.einsum('bhncd,bhnsd->bhncs', qe, ke), 0.0)
    o_intra = jnp.einsum('bhncs,bhnse->bhnce', a, v)
    upd = jnp.einsum('bhncd,bhnce->nbhde', kd, v)
    decay = jnp.exp(b_last[:, :, :, 0, :]).transpose(2, 0, 1, 3)

    def step(s, inp):
        dcy, u = inp
        return dcy[..., None] * s + u, s

    s_final, s_in = lax.scan(step, s0, (decay, upd))
    o_inter = jnp.einsum('bhncd,nbhde->bhnce', qe, s_in)
    return (o_intra + o_inter).reshape(b_, h_, l_, dv), s_final


def gla_bidirectional(q, k, v, lg_f, lg_b, s0_f, s0_b):
    o_f, s_f = gla_chunked(q, k, v, lg_f, s0_f, strict=False)
    flip = lambda t: jnp.flip(t, axis=2)
    o_b, s_b = gla_chunked(flip(q), flip(k), flip(v), flip(lg_b), s0_b, strict=True)
    return o_f + flip(o_b), s_f, s_b


def gla_output(o, r, norm_g, w_out, dtype):
    b_, h_, l_, hv = o.shape
    o = o.transpose(0, 2, 1, 3)
    o = o * lax.rsqrt(jnp.mean(o * o, axis=-1, keepdims=True) + NORM_EPS)
    o = o * norm_g.astype(jnp.float32).reshape(GLA_HEADS, GLA_HEAD_V)
    o = o.reshape(b_, l_, h_ * hv) * jax.nn.silu(r)
    return o.astype(dtype) @ w_out


def conv_module(h, w_pw1, b_pw1, w_dw, b_dw, ln_g, ln_b, w_pw2, b_pw2):
    u = h @ w_pw1 + b_pw1
    u = u[..., :D_MODEL] * jax.nn.sigmoid(u[..., D_MODEL:])
    u = lax.conv_general_dilated(
        u, w_dw[:, None, :].astype(u.dtype), window_strides=(1,),
        padding=[(CONV_WIDTH // 2, CONV_WIDTH // 2)],
        dimension_numbers=('NWC', 'WIO', 'NWC'), feature_group_count=D_MODEL) + b_dw
    u = jax.nn.silu(layernorm(u, ln_g, ln_b))
    return u @ w_pw2 + b_pw2


def swiglu(h, w_in, w_out):
    u = h @ w_in
    return (jax.nn.silu(u[..., :FFN_HIDDEN]) * u[..., FFN_HIDDEN:]) @ w_out


def setup_inputs(seed: int = 0) -> dict:
    key = jax.random.key(seed)
    ks = jax.random.split(key, 32)
    nrm = lambda k, shape, scale: jax.random.normal(k, shape, jnp.float32) * scale
    D = D_MODEL
    G = N_GLA_LAYERS
    C = N_CONV_LAYERS
    return {
        'x': nrm(ks[0], (BATCH, SEQ, D), 1.0),
        'c': nrm(ks[1], (BATCH, D), 1.0),
        'ctx': nrm(ks[2], (BATCH, CTX_LEN, D), 1.0),
        'c_ctx': nrm(ks[3], (D,), 1.0),
        'w_mod': nrm(ks[4], (DEPTH, D, 6 * D), 0.5 * D ** -0.5),
        'b_mod': nrm(ks[5], (DEPTH, 6 * D), 0.02),
        'norm_mix_g': 1.0 + nrm(ks[6], (DEPTH, D), 0.02),
        'norm_ffn_g': 1.0 + nrm(ks[7], (DEPTH, D), 0.02),
        'gla_w_in': nrm(ks[8], (G, D, GLA_IN), D ** -0.5),
        'gla_wa_f': nrm(ks[9], (G, GLA_GATE_RANK, GLA_DK), GLA_GATE_RANK ** -0.5),
        'gla_ba_f': nrm(ks[10], (G, GLA_DK), 0.1),
        'gla_wa_b': nrm(ks[11], (G, GLA_GATE_RANK, GLA_DK), GLA_GATE_RANK ** -0.5),
        'gla_ba_b': nrm(ks[12], (G, GLA_DK), 0.1),
        'gla_norm_g': 1.0 + nrm(ks[13], (G, GLA_DV), 0.02),
        'gla_w_out': nrm(ks[14], (G, GLA_DV, D), GLA_DV ** -0.5),
        'conv_w_pw1': nrm(ks[15], (C, D, 2 * D), D ** -0.5),
        'conv_b_pw1': nrm(ks[16], (C, 2 * D), 0.02),
        'conv_w_dw': nrm(ks[17], (C, CONV_WIDTH, D), CONV_WIDTH ** -0.5),
        'conv_b_dw': nrm(ks[18], (C, D), 0.02),
        'conv_ln_g': 1.0 + nrm(ks[19], (C, D), 0.02),
        'conv_ln_b': nrm(ks[20], (C, D), 0.02),
        'conv_w_pw2': nrm(ks[21], (C, D, D), D ** -0.5),
        'conv_b_pw2': nrm(ks[22], (C, D), 0.02),
        'ffn_w_in': nrm(ks[23], (DEPTH, D, 2 * FFN_HIDDEN), D ** -0.5),
        'ffn_w_out': nrm(ks[24], (DEPTH, FFN_HIDDEN, D), FFN_HIDDEN ** -0.5),
        'final_norm_g': 1.0 + nrm(ks[25], (D,), 0.02),
    }


def reference(x, c, ctx, c_ctx, w_mod, b_mod, norm_mix_g, norm_ffn_g,
              gla_w_in, gla_wa_f, gla_ba_f, gla_wa_b, gla_ba_b, gla_norm_g, gla_w_out,
              conv_w_pw1, conv_b_pw1, conv_w_dw, conv_b_dw, conv_ln_g, conv_ln_b, conv_w_pw2, conv_b_pw2,
              ffn_w_in, ffn_w_out, final_norm_g):
    batch = x.shape[0]
    for i in range(DEPTH):
        last = i == DEPTH - 1
        j = i // N_MIXERS
        col_major = j % 2 == 1
        mod_lat = jax.nn.silu(c) @ w_mod[i] + b_mod[i]
        mod_ctx = jax.nn.silu(c_ctx) @ w_mod[i] + b_mod[i]
        sh1, sc1, gt1, sh2, sc2, gt2 = jnp.split(mod_lat[:, None, :], 6, axis=-1)
        csh1, csc1, cgt1, csh2, csc2, cgt2 = jnp.split(mod_ctx, 6, axis=-1)
        h_lat = to_grid_order(modulate(rmsnorm(x, norm_mix_g[i]), sh1, sc1), col_major)
        y_ctx = None
        if i % N_MIXERS == 0:
            p = (gla_w_in[j], gla_wa_f[j], gla_ba_f[j], gla_wa_b[j], gla_ba_b[j])
            h_ctx = modulate(rmsnorm(ctx, norm_mix_g[i]), csh1, csc1)
            qc, kc, vc, rc, gfc, gbc = gla_project(h_ctx, *p)
            ql, kl, vl, rl, gfl, gbl = gla_project(h_lat, *p)
            zero = jnp.zeros((batch, GLA_HEADS, GLA_HEAD_K, GLA_HEAD_V), jnp.float32)
            o_ctx, s_f, s_b = gla_bidirectional(qc, kc, vc, gfc, gbc, zero, zero)
            o_lat, _, _ = gla_bidirectional(ql, kl, vl, gfl, gbl, s_f, s_b)
            y_lat = gla_output(o_lat, rl, gla_norm_g[j], gla_w_out[j], x.dtype)
            if not last:
                y_ctx = gla_output(o_ctx, rc, gla_norm_g[j], gla_w_out[j], ctx.dtype)
        else:
            p = (conv_w_pw1[j], conv_b_pw1[j], conv_w_dw[j], conv_b_dw[j],
                 conv_ln_g[j], conv_ln_b[j], conv_w_pw2[j], conv_b_pw2[j])
            y_lat = conv_module(h_lat, *p)
            if not last:
                h_ctx = modulate(rmsnorm(ctx, norm_mix_g[i]), csh1, csc1)
                y_ctx = conv_module(h_ctx, *p)
        x = x + gt1 * from_grid_order(y_lat, col_major)
        x = x + gt2 * swiglu(modulate(rmsnorm(x, norm_ffn_g[i]), sh2, sc2), ffn_w_in[i], ffn_w_out[i])
        if not last:
            ctx = ctx + cgt1 * y_ctx
            ctx = ctx + cgt2 * swiglu(modulate(rmsnorm(ctx, norm_ffn_g[i]), csh2, csc2), ffn_w_in[i], ffn_w_out[i])
    return rmsnorm(x, final_norm_g)
```

```python
import functools

import jax
import jax.numpy as jnp
from jax import lax
from jax.experimental import pallas as pl
from jax.experimental.pallas import tpu as pltpu

D = 1024
DEPTH = 4
GRID_W = 64
GRID_ROWS = 64
N_MIXERS = 2
HEADS = 4
DK = D // 2
DV = D
HEAD_K = DK // HEADS
HEAD_V = DV // HEADS
GATE_RANK = 16
GATE_TAU = 16.0
CHUNK = 64
CONV_W = 31
CONV_HALF = CONV_W // 2
FFN_H = 2816
EPS = 1e-6

LANES = 128
SUBLANES = 8
HALO = 2 * SUBLANES
MOD_ROWS = 16
FFN_CK = 256
TM_LAT = 512
VMEM_LIMIT = 56 * 1024 * 1024

F32 = jnp.float32
BF16 = jnp.bfloat16


def _cparams(n_grid):
    return pltpu.CompilerParams(
        dimension_semantics=("arbitrary",) * n_grid, vmem_limit_bytes=VMEM_LIMIT)


def _sigmoid(x):
    return 1.0 / (1.0 + jnp.exp(-x))


def _silu(x):
    return x * _sigmoid(x)


def _log_sigmoid(x):
    return jnp.minimum(x, 0.0) - jnp.log(1.0 + jnp.exp(-jnp.abs(x)))


def _rmsnorm(x, g):
    return x * lax.rsqrt(jnp.mean(x * x, axis=-1, keepdims=True) + EPS) * g


def _mod_slice(mod_ref, j):
    return mod_ref[:, j * D:(j + 1) * D]


def _load_tokens(ref, col_major):
    if not col_major:
        return ref[...]
    ncol = ref.shape[1] // D
    return jnp.concatenate([ref[:, k * D:(k + 1) * D] for k in range(ncol)], axis=0)


def _store_tokens(ref, val, col_major):
    if not col_major:
        ref[...] = val
        return
    ncol = ref.shape[1] // D
    for k in range(ncol):
        ref[:, k * D:(k + 1) * D] = val[k * GRID_ROWS:(k + 1) * GRID_ROWS, :]


def _dot(a, b):
    return jnp.dot(a, b, preferred_element_type=F32)


def _tok_spec(tm, width, col_major=False):
    if col_major:
        return pl.BlockSpec((None, GRID_ROWS, (tm // GRID_ROWS) * width), lambda b, t: (b, 0, t))
    return pl.BlockSpec((None, tm, width), lambda b, t: (b, t, 0))


def _mod_spec(ctx_row):
    if ctx_row is None:
        return pl.BlockSpec((None, 1, 6 * D), lambda b, t: (b, 0, 0))
    return pl.BlockSpec((None, 1, 6 * D), lambda b, t: (ctx_row, 0, 0))


def _full_spec(shape):
    nd = len(shape)
    return pl.BlockSpec(shape, lambda b, t: (0,) * nd)


def _x_view(x, col_major):
    if not col_major:
        return x
    b = x.shape[0]
    return x.reshape(b, GRID_ROWS, GRID_W * D)


def _x_unview(xv, col_major, seq):
    if not col_major:
        return xv
    return xv.reshape(xv.shape[0], seq, D)


def _mod_kernel(cc_ref, w_ref, b_ref, o_ref):
    s = _silu(cc_ref[...]).astype(BF16)
    o_ref[...] = _dot(s, w_ref[...].astype(BF16)) + b_ref[...]


def _modulation(cc, w_mod, b_mod):
    tn = 1536
    return pl.pallas_call(
        _mod_kernel,
        out_shape=jax.ShapeDtypeStruct((DEPTH, MOD_ROWS, 6 * D), F32),
        grid=(DEPTH, 6 * D // tn),
        in_specs=[
            pl.BlockSpec((MOD_ROWS, D), lambda i, n: (0, 0)),
            pl.BlockSpec((None, D, tn), lambda i, n: (i, 0, n)),
            pl.BlockSpec((None, 1, tn), lambda i, n: (i, 0, n)),
        ],
        out_specs=pl.BlockSpec((None, MOD_ROWS, tn), lambda i, n: (i, 0, n)),
        compiler_params=_cparams(2),
        name="modulation",
    )(cc, w_mod, b_mod.reshape(DEPTH, 1, 6 * D))


def _gla_proj_kernel(x_ref, mod_ref, g_ref, w_ref, wg_ref, wa_ref, ba_ref,
                     q_ref, k_ref, v_ref, r_ref, lf_ref, lb_ref, *, col_major):
    x = _load_tokens(x_ref, col_major)
    h = _rmsnorm(x, g_ref[...]) * (1.0 + _mod_slice(mod_ref, 1)) + _mod_slice(mod_ref, 0)
    hb = h.astype(BF16)
    q_ref[...] = _dot(hb, w_ref[:, 0:DK]) * (HEAD_K ** -0.5)
    k_ref[...] = _dot(hb, w_ref[:, DK:2 * DK])
    v_ref[...] = _dot(hb, w_ref[:, 2 * DK:2 * DK + DV])
    r_ref[...] = _dot(hb, w_ref[:, 2 * DK + DV:2 * DK + 2 * DV])
    zg = _dot(hb, wg_ref[...])
    lg = _log_sigmoid(_dot(zg.astype(BF16), wa_ref[...]) + ba_ref[...]) * (1.0 / GATE_TAU)
    lf_ref[...] = lg[:, 0:DK]
    lb_ref[...] = lg[:, DK:2 * DK]


def _gla_project(xv, mods, ctx_row, norm_g, w_main, w_gate, wa, ba, *, seq, tm, col_major):
    batch = xv.shape[0]
    outs = [jax.ShapeDtypeStruct((batch, seq, w), F32) for w in (DK, DK, DV, DV, DK, DK)]
    return pl.pallas_call(
        functools.partial(_gla_proj_kernel, col_major=col_major),
        out_shape=outs,
        grid=(batch, seq // tm),
        in_specs=[
            _tok_spec(tm, D, col_major), _mod_spec(ctx_row), _full_spec((1, D)),
            _full_spec(w_main.shape), _full_spec(w_gate.shape), _full_spec(wa.shape),
            _full_spec(ba.shape),
        ],
        out_specs=[_tok_spec(tm, w) for w in (DK, DK, DV, DV, DK, DK)],
        compiler_params=_cparams(2),
        name="gla_proj",
    )(xv, mods, norm_g, w_main, w_gate, wa, ba)


def _gla_scan_kernel(qc_ref, kc_ref, vc_ref, fc_ref, bc_ref,
                     ql_ref, kl_ref, vl_ref, fl_ref, bl_ref, oc_ref, ol_ref):
    row = lax.broadcasted_iota(jnp.int32, (CHUNK, CHUNK), 0)
    col = lax.broadcasted_iota(jnp.int32, (CHUNK, CHUNK), 1)
    tri_f = (row >= col).astype(F32)
    tri_b = (col >= row).astype(F32)
    mask_f = row >= col
    mask_b = col > row
    nt = (((1,), (1,)), ((), ()))
    tn = (((0,), (0,)), ((), ()))

    def chunk(q, k, v, lg, st, fwd):
        cum = jnp.dot(tri_f if fwd else tri_b, lg, precision=lax.Precision.HIGHEST,
                      preferred_element_type=F32)
        tot = cum[CHUNK - 1:CHUNK, :] if fwd else cum[0:1, :]
        qe = (q * jnp.exp(cum)).astype(BF16)
        ke = (k * jnp.exp(-cum)).astype(BF16)
        kd = (k * jnp.exp(tot - cum)).astype(BF16)
        vb = v.astype(BF16)
        a = lax.dot_general(qe, ke, nt, preferred_element_type=F32)
        a = jnp.where(mask_f if fwd else mask_b, a, 0.0).astype(BF16)
        o = _dot(a, vb) + lax.dot_general(qe, st.astype(BF16), nt, preferred_element_type=F32)
        st = st * jnp.exp(tot) + lax.dot_general(vb, kd, tn, preferred_element_type=F32)
        return o, st

    n_ctx = qc_ref.shape[0] // CHUNK
    n_lat = ql_ref.shape[0] // CHUNK

    def ctx_chunk(c, st, fwd):
        sl = slice(c * CHUNK, (c + 1) * CHUNK)
        g_ref = fc_ref if fwd else bc_ref
        return chunk(qc_ref[sl, :], kc_ref[sl, :], vc_ref[sl, :], g_ref[sl, :], st, fwd)

    def lat_chunk(c, st, fwd):
        sl = pl.ds(pl.multiple_of(c * CHUNK, CHUNK), CHUNK)
        g_ref = fl_ref if fwd else bl_ref
        o, st = chunk(ql_ref[sl, :], kl_ref[sl, :], vl_ref[sl, :], g_ref[sl, :], st, fwd)
        return sl, o, st

    st = jnp.zeros((HEAD_V, HEAD_K), F32)
    for c in range(n_ctx):
        o, st = ctx_chunk(c, st, True)
        oc_ref[c * CHUNK:(c + 1) * CHUNK, :] = o

    def fwd_body(c, st):
        sl, o, st = lat_chunk(c, st, True)
        ol_ref[sl, :] = o
        return st

    lax.fori_loop(0, n_lat, fwd_body, st)

    st = jnp.zeros((HEAD_V, HEAD_K), F32)
    for c in reversed(range(n_ctx)):
        o, st = ctx_chunk(c, st, False)
        oc_ref[c * CHUNK:(c + 1) * CHUNK, :] += o

    def bwd_body(j, st):
        sl, o, st = lat_chunk(n_lat - 1 - j, st, False)
        ol_ref[sl, :] += o
        return st

    lax.fori_loop(0, n_lat, bwd_body, st)


def _gla_scan(ctx_parts, lat_parts):
    batch, n_ctx, _ = ctx_parts[0].shape
    n_lat = lat_parts[0].shape[1]

    def specs(seq):
        kspec = pl.BlockSpec((None, seq, HEAD_K), lambda b, h: (b, 0, h))
        vspec = pl.BlockSpec((None, seq, HEAD_V), lambda b, h: (b, 0, h))
        return [kspec, kspec, vspec, kspec, kspec], vspec

    in_c, out_c = specs(n_ctx)
    in_l, out_l = specs(n_lat)
    return pl.pallas_call(
        _gla_scan_kernel,
        out_shape=[jax.ShapeDtypeStruct((batch, n_ctx, DV), F32),
                   jax.ShapeDtypeStruct((batch, n_lat, DV), F32)],
        grid=(batch, HEADS),
        in_specs=in_c + in_l,
        out_specs=[out_c, out_l],
        compiler_params=_cparams(2),
        name="gla_scan",
    )(*ctx_parts, *lat_parts)


def _gla_out_kernel(o_ref, r_ref, x_ref, mod_ref, ng_ref, w_ref, xo_ref, *, col_major):
    heads = []
    for h in range(HEADS):
        oh = o_ref[:, h * HEAD_V:(h + 1) * HEAD_V]
        heads.append(_rmsnorm(oh, ng_ref[:, h * HEAD_V:(h + 1) * HEAD_V]))
    on = jnp.concatenate(heads, axis=1) * _silu(r_ref[...])
    y = _dot(on.astype(BF16), w_ref[...])
    x = _load_tokens(x_ref, col_major)
    _store_tokens(xo_ref, x + _mod_slice(mod_ref, 2) * y, col_major)


def _gla_output(o, r, xv, mods, ctx_row, norm_g, w_out, *, seq, tm, col_major):
    batch = xv.shape[0]
    return pl.pallas_call(
        functools.partial(_gla_out_kernel, col_major=col_major),
        out_shape=jax.ShapeDtypeStruct(xv.shape, F32),
        grid=(batch, seq // tm),
        in_specs=[
            _tok_spec(tm, DV), _tok_spec(tm, DV), _tok_spec(tm, D, col_major), _mod_spec(ctx_row),
            _full_spec((1, DV)), _full_spec(w_out.shape),
        ],
        out_specs=_tok_spec(tm, D, col_major),
        compiler_params=_cparams(2),
        name="gla_out",
    )(o, r, xv, mods, norm_g, w_out)


def _conv_pw1_kernel(x_ref, mod_ref, g_ref, w_ref, b_ref, u_ref, *, col_major):
    x = _load_tokens(x_ref, col_major)
    h = _rmsnorm(x, g_ref[...]) * (1.0 + _mod_slice(mod_ref, 1)) + _mod_slice(mod_ref, 0)
    hb = h.astype(BF16)
    a = _dot(hb, w_ref[:, 0:D]) + b_ref[:, 0:D]
    gate = _dot(hb, w_ref[:, D:2 * D]) + b_ref[:, D:2 * D]
    u_ref[...] = a * _sigmoid(gate)


def _conv_pw1(xv, mods, ctx_row, norm_g, w_pw1, b_pw1, *, seq, tm, col_major):
    batch = xv.shape[0]
    return pl.pallas_call(
        functools.partial(_conv_pw1_kernel, col_major=col_major),
        out_shape=jax.ShapeDtypeStruct((batch, seq, D), F32),
        grid=(batch, seq // tm),
        in_specs=[
            _tok_spec(tm, D, col_major), _mod_spec(ctx_row), _full_spec((1, D)),
            _full_spec(w_pw1.shape), _full_spec(b_pw1.shape),
        ],
        out_specs=_tok_spec(tm, D),
        compiler_params=_cparams(2),
        name="conv_pw1",
    )(xv, mods, norm_g, w_pw1, b_pw1)


def _conv_out_kernel(up_ref, um_ref, un_ref, x_ref, mod_ref, wdw_ref, bdw_ref, lng_ref, lnb_ref,
                     w_ref, b_ref, xo_ref, buf_ref, *, col_major):
    tm = um_ref.shape[0]
    t = pl.program_id(1)
    last = pl.num_programs(1) - 1
    buf_ref[0:HALO, :] = jnp.where(t > 0, up_ref[...], 0.0)
    buf_ref[HALO:HALO + tm, :] = um_ref[...]
    buf_ref[HALO + tm:2 * HALO + tm, :] = jnp.where(t < last, un_ref[...], 0.0)
    acc = jnp.zeros((tm, D), F32) + bdw_ref[...]
    for k in range(CONV_W):
        off = HALO - CONV_HALF + k
        acc = acc + wdw_ref[k:k + 1, :] * buf_ref[off:off + tm, :]
    mu = jnp.mean(acc, axis=-1, keepdims=True)
    xc = acc - mu
    y = xc * lax.rsqrt(jnp.mean(xc * xc, axis=-1, keepdims=True) + EPS) * lng_ref[...] + lnb_ref[...]
    y = _dot(_silu(y).astype(BF16), w_ref[...]) + b_ref[...]
    x = _load_tokens(x_ref, col_major)
    _store_tokens(xo_ref, x + _mod_slice(mod_ref, 2) * y, col_major)


def _conv_output(u, xv, mods, ctx_row, w_dw, b_dw, ln_g, ln_b, w_pw2, b_pw2, *, seq, tm, col_major):
    batch = xv.shape[0]
    per = tm // HALO
    n_halo = seq // HALO
    prev_spec = pl.BlockSpec((None, HALO, D), lambda b, t: (b, jnp.maximum(t * per - 1, 0), 0))
    next_spec = pl.BlockSpec((None, HALO, D), lambda b, t: (b, jnp.minimum((t + 1) * per, n_halo - 1), 0))
    return pl.pallas_call(
        functools.partial(_conv_out_kernel, col_major=col_major),
        out_shape=jax.ShapeDtypeStruct(xv.shape, F32),
        grid=(batch, seq // tm),
        in_specs=[
            prev_spec, _tok_spec(tm, D), next_spec, _tok_spec(tm, D, col_major), _mod_spec(ctx_row),
            _full_spec(w_dw.shape), _full_spec((1, D)), _full_spec((1, D)), _full_spec((1, D)),
            _full_spec(w_pw2.shape), _full_spec((1, D)),
        ],
        out_specs=_tok_spec(tm, D, col_major),
        scratch_shapes=[pltpu.VMEM((tm + 2 * HALO, D), F32)],
        compiler_params=_cparams(2),
        name="conv_out",
    )(u, u, u, xv, mods, w_dw, b_dw, ln_g, ln_b, w_pw2, b_pw2)


def _ffn_kernel(x_ref, mod_ref, g_ref, w_in_ref, w_out_ref, fg_ref, xo_ref, *, final_norm):
    x = x_ref[...]
    h = _rmsnorm(x, g_ref[...]) * (1.0 + _mod_slice(mod_ref, 4)) + _mod_slice(mod_ref, 3)
    hb = h.astype(BF16)
    acc = jnp.zeros(x.shape, F32)
    for j in range(FFN_H // FFN_CK):
        lo = j * FFN_CK
        u1 = _dot(hb, w_in_ref[:, lo:lo + FFN_CK])
        u2 = _dot(hb, w_in_ref[:, FFN_H + lo:FFN_H + lo + FFN_CK])
        acc = acc + _dot((_silu(u1) * u2).astype(BF16), w_out_ref[lo:lo + FFN_CK, :])
    out = x + _mod_slice(mod_ref, 5) * acc
    if final_norm:
        out = _rmsnorm(out, fg_ref[...])
    xo_ref[...] = out


def _ffn(x, mods, ctx_row, norm_g, w_in, w_out, final_g, *, seq, tm, final_norm):
    batch = x.shape[0]
    return pl.pallas_call(
        functools.partial(_ffn_kernel, final_norm=final_norm),
        out_shape=jax.ShapeDtypeStruct(x.shape, F32),
        grid=(batch, seq // tm),
        in_specs=[
            _tok_spec(tm, D), _mod_spec(ctx_row), _full_spec((1, D)),
            _full_spec(w_in.shape), _full_spec(w_out.shape), _full_spec((1, D)),
        ],
        out_specs=_tok_spec(tm, D),
        compiler_params=_cparams(2),
        name="ffn",
    )(x, mods, norm_g, w_in, w_out, final_g)


def kernel(x, c, ctx, c_ctx, w_mod, b_mod, norm_mix_g, norm_ffn_g, gla_w_in, gla_wa_f, gla_ba_f,
           gla_wa_b, gla_ba_b, gla_norm_g, gla_w_out, conv_w_pw1, conv_b_pw1, conv_w_dw, conv_b_dw,
           conv_ln_g, conv_ln_b, conv_w_pw2, conv_b_pw2, ffn_w_in, ffn_w_out, final_norm_g):
    batch, seq, _ = x.shape
    n_ctx = ctx.shape[1]
    ctx_row = batch
    assert batch + 1 <= MOD_ROWS and seq == GRID_ROWS * GRID_W

    cc = jnp.concatenate([c, c_ctx[None, :], jnp.zeros((MOD_ROWS - batch - 1, D), F32)], axis=0)
    mods_all = _modulation(cc, w_mod, b_mod).reshape(DEPTH, MOD_ROWS, 1, 6 * D)

    row = lambda v: v.reshape(1, -1)
    final_g = row(final_norm_g)

    for i in range(DEPTH):
        last = i == DEPTH - 1
        j = i // N_MIXERS
        col_major = j % 2 == 1
        mods = mods_all[i]
        xv = _x_view(x, col_major)
        lat = dict(seq=seq, tm=TM_LAT, col_major=col_major)
        cx = dict(seq=n_ctx, tm=n_ctx, col_major=False)
        if i % N_MIXERS == 0:
            o1 = 2 * DK + 2 * DV
            w_main = gla_w_in[j][:, :o1].astype(BF16)
            w_gate = jnp.pad(gla_w_in[j][:, o1:], ((0, 0), (0, LANES - 2 * GATE_RANK))).astype(BF16)
            wa = jnp.zeros((LANES, 2 * DK), F32)
            wa = wa.at[0:GATE_RANK, 0:DK].set(gla_wa_f[j])
            wa = wa.at[GATE_RANK:2 * GATE_RANK, DK:2 * DK].set(gla_wa_b[j]).astype(BF16)
            ba = jnp.concatenate([gla_ba_f[j], gla_ba_b[j]]).reshape(1, 2 * DK)
            pw = (row(norm_mix_g[i]), w_main, w_gate, wa, ba)
            ctx_parts = _gla_project(ctx, mods, ctx_row, *pw, **cx)
            lat_parts = _gla_project(xv, mods, None, *pw, **lat)
            pick = lambda parts: [parts[0], parts[1], parts[2], parts[4], parts[5]]
            o_ctx, o_lat = _gla_scan(pick(ctx_parts), pick(lat_parts))
            ow = (row(gla_norm_g[j]), gla_w_out[j].astype(BF16))
            x = _x_unview(_gla_output(o_lat, lat_parts[3], xv, mods, None, *ow, **lat), col_major, seq)
            if not last:
                ctx = _gla_output(o_ctx, ctx_parts[3], ctx, mods, ctx_row, *ow, **cx)
        else:
            pw1 = (row(norm_mix_g[i]), conv_w_pw1[j].astype(BF16), row(conv_b_pw1[j]))
            pw2 = (jnp.pad(conv_w_dw[j], ((0, 1), (0, 0))), row(conv_b_dw[j]), row(conv_ln_g[j]),
                   row(conv_ln_b[j]), conv_w_pw2[j].astype(BF16), row(conv_b_pw2[j]))
            u_lat = _conv_pw1(xv, mods, None, *pw1, **lat)
            x = _x_unview(_conv_output(u_lat, xv, mods, None, *pw2, **lat), col_major, seq)
            if not last:
                u_ctx = _conv_pw1(ctx, mods, ctx_row, *pw1, **cx)
                ctx = _conv_output(u_ctx, ctx, mods, ctx_row, *pw2, **cx)
        fw = (row(norm_ffn_g[i]), ffn_w_in[i].astype(BF16), ffn_w_out[i].astype(BF16), final_g)
        x = _ffn(x, mods, None, *fw, seq=seq, tm=TM_LAT, final_norm=last)
        if not last:
            ctx = _ffn(ctx, mods, ctx_row, *fw, seq=n_ctx, tm=n_ctx, final_norm=False)
    return x
```
